```python
import jax, jax.numpy as jnp
from jax import lax
import numpy as np

D_MODEL = 2048
BATCH = 16
SEQ = 2048
DEPTH = 1
DEC_BATCH = 8
DEC_SEQ = 32
PAST_LEN = 1024

CHUNK = 64
MIX_WIDTH = D_MODEL
D_CONV = MIX_WIDTH // 2
N_CONV_GROUPS = 16
D_GMLP = MIX_WIDTH - D_CONV
N_GMLP_GROUPS = 8
GMLP_GROUP_DIM = D_GMLP // N_GMLP_GROUPS
GMLP_LEN = 128
CONV_WIDTH = 3
D_FF = 5632
D_IN_PROJ = 3 * D_CONV + 2 * D_GMLP
EPS = 1e-6

kernel_name = 'hymba_conv_gmlp_macaron_stream_step'


def rms_norm(x, g):
    xf = x.astype(jnp.float32)
    y = xf * lax.rsqrt(jnp.mean(xf * xf, axis=-1, keepdims=True) + EPS)
    return (y * g.astype(jnp.float32)).astype(x.dtype)


def swiglu_half_step(x, g, w_gate, w_up, w_down):
    h = rms_norm(x, g)
    a = jax.nn.silu(h @ w_gate) * (h @ w_up)
    return x + 0.5 * (a @ w_down)


def causal_conv3(zp, w):
    T = zp.shape[1] - (CONV_WIDTH - 1)
    return w[0] * zp[:, 0:T] + w[1] * zp[:, 1:T + 1] + w[2] * zp[:, 2:T + 2]


def chunk_causal_mask():
    blk = np.arange(GMLP_LEN) // CHUNK
    return jnp.asarray(blk[:, None] >= blk[None, :])


def gmlp_spatial_gate(u, v, w_s, b_s):
    bsz, T, _ = v.shape
    n_chunks = -(-T // GMLP_LEN)
    pad = n_chunks * GMLP_LEN - T
    vp = jnp.pad(v, ((0, 0), (0, pad), (0, 0)))
    vp = vp.reshape(bsz, n_chunks, GMLP_LEN, N_GMLP_GROUPS, GMLP_GROUP_DIM)
    w = jnp.where(chunk_causal_mask()[None], w_s, jnp.zeros_like(w_s)).astype(v.dtype)
    mixed = jnp.einsum('gij,bcjgd->bcigd', w, vp) + b_s.T.astype(v.dtype)[None, None, :, :, None]
    mixed = mixed.reshape(bsz, n_chunks * GMLP_LEN, D_GMLP)[:, :T]
    return u * mixed


def parallel_mixing(x, conv_hist, norm_g, w_in, conv_w, v_norm_g, w_s, b_s, conv_out_g, gmlp_out_g, w_o):
    h = rms_norm(x, norm_g)
    proj = h @ w_in
    gate_b, gate_c, xc, u, v = jnp.split(
        proj, [D_CONV, 2 * D_CONV, 3 * D_CONV, 3 * D_CONV + D_GMLP], axis=-1)
    z = gate_c * xc
    zp = jnp.concatenate([conv_hist.astype(z.dtype), z], axis=1)
    y_conv = gate_b * causal_conv3(zp, conv_w)
    new_conv = zp[:, -(CONV_WIDTH - 1):]
    u = jax.nn.gelu(u)
    v = rms_norm(jax.nn.gelu(v), v_norm_g)
    y_gmlp = gmlp_spatial_gate(u, v, w_s, b_s)
    y = jnp.concatenate([rms_norm(y_conv, conv_out_g), rms_norm(y_gmlp, gmlp_out_g)], axis=-1) @ w_o
    return x + y, new_conv, v


def setup_inputs(seed: int = 0) -> dict:
    key = jax.random.key(seed)
    ks = iter(jax.random.split(key, 32))

    def nrm(shape, scale):
        return jax.random.normal(next(ks), shape, jnp.float32) * scale

    def gain(shape):
        return 1.0 + nrm(shape, 0.01)

    return {
        'x_prompt': nrm((BATCH, SEQ, D_MODEL), 1.0),
        'x_sample': nrm((DEC_BATCH, DEC_SEQ, D_MODEL), 1.0),
        'cache_conv': nrm((DEPTH, DEC_BATCH, CONV_WIDTH - 1, D_CONV), 1.0),
        'ffn1_norm': gain((DEPTH, D_MODEL)),
        'ffn1_w_gate': nrm((DEPTH, D_MODEL, D_FF), D_MODEL ** -0.5),
        'ffn1_w_up': nrm((DEPTH, D_MODEL, D_FF), D_MODEL ** -0.5),
        'ffn1_w_down': nrm((DEPTH, D_FF, D_MODEL), D_FF ** -0.5),
        'mix_norm': gain((DEPTH, D_MODEL)),
        'w_in': nrm((DEPTH, D_MODEL, D_IN_PROJ), D_MODEL ** -0.5),
        'conv_w': nrm((DEPTH, CONV_WIDTH, D_CONV), CONV_WIDTH ** -0.5),
        'gmlp_v_norm': gain((DEPTH, D_GMLP)),
        'gmlp_w_s': nrm((DEPTH, N_GMLP_GROUPS, GMLP_LEN, GMLP_LEN), GMLP_LEN ** -0.5),
        'gmlp_b': 1.0 + nrm((DEPTH, N_GMLP_GROUPS, GMLP_LEN), 0.1),
        'conv_out_norm': gain((DEPTH, D_CONV)),
        'gmlp_out_norm': gain((DEPTH, D_GMLP)),
        'w_o': nrm((DEPTH, MIX_WIDTH, D_MODEL), MIX_WIDTH ** -0.5),
        'ffn2_norm': gain((DEPTH, D_MODEL)),
        'ffn2_w_gate': nrm((DEPTH, D_MODEL, D_FF), D_MODEL ** -0.5),
        'ffn2_w_up': nrm((DEPTH, D_MODEL, D_FF), D_MODEL ** -0.5),
        'ffn2_w_down': nrm((DEPTH, D_FF, D_MODEL), D_FF ** -0.5),
        'final_norm': gain((D_MODEL,)),
    }


def reference(x_prompt, x_sample, cache_conv, ffn1_norm, ffn1_w_gate, ffn1_w_up, ffn1_w_down,
              mix_norm, w_in, conv_w, gmlp_v_norm, gmlp_w_s, gmlp_b, conv_out_norm, gmlp_out_norm,
              w_o, ffn2_norm, ffn2_w_gate, ffn2_w_up, ffn2_w_down, final_norm):
    def trunk(x, conv_hist_all):
        conv_states, v_rows = [], []
        for l in range(DEPTH):
            x = swiglu_half_step(x, ffn1_norm[l], ffn1_w_gate[l], ffn1_w_up[l], ffn1_w_down[l])
            x, conv_state, v = parallel_mixing(
                x, conv_hist_all[l], mix_norm[l], w_in[l], conv_w[l], gmlp_v_norm[l],
                gmlp_w_s[l], gmlp_b[l], conv_out_norm[l], gmlp_out_norm[l], w_o[l])
            x = swiglu_half_step(x, ffn2_norm[l], ffn2_w_gate[l], ffn2_w_up[l], ffn2_w_down[l])
            conv_states.append(conv_state)
            v_rows.append(v)
        return rms_norm(x, final_norm), jnp.stack(conv_states), jnp.stack(v_rows)

    zero_hist = jnp.zeros((DEPTH, x_prompt.shape[0], CONV_WIDTH - 1, D_CONV), x_prompt.dtype)
    y_prompt, conv_state_prompt, _ = trunk(x_prompt, zero_hist)
    y_sample, conv_state_sample, gmlp_v_sample = trunk(x_sample, cache_conv)
    return (y_prompt, y_sample, conv_state_prompt, conv_state_sample, gmlp_v_sample)
```

```python
import functools

import jax
import jax.numpy as jnp
from jax import lax
from jax.experimental import pallas as pl
from jax.experimental.pallas import tpu as pltpu

D_MODEL = 2048
D_FF = 5632
D_CONV = 1024
D_GMLP = 1024
N_GMLP_GROUPS = 8
GMLP_GROUP_DIM = D_GMLP // N_GMLP_GROUPS
GMLP_LEN = 128
CHUNK = 64
CONV_WIDTH = 3
EPS = 1e-6

N_MIX_STEPS = 5

ROW_CHUNK = 64

V7X_VMEM_BYTES = 64 * 1024 * 1024
VMEM_LIMIT_BYTES = V7X_VMEM_BYTES - 4 * 1024 * 1024

F32 = jnp.float32
BF16 = jnp.bfloat16


def _rms_normalize(xf, gain):
    scale = lax.rsqrt(jnp.mean(xf * xf, axis=-1, keepdims=True) + EPS)
    return xf * scale * gain


def _dot(a, b):
    return jnp.dot(a, b, preferred_element_type=F32)


def _ffn_body(x_ref, g_ref, wg_ref, wu_ref, wd_ref, fn_ref, o_ref, h_ref, *, n_ff_steps, out_cols):
    j = pl.program_id(1)
    tm = x_ref.shape[0]

    def for_row_chunks(fn):
        def step(r, carry):
            fn(pl.ds(pl.multiple_of(r * ROW_CHUNK, ROW_CHUNK), ROW_CHUNK))
            return carry
        lax.fori_loop(0, tm // ROW_CHUNK, step, 0)

    @pl.when(j == 0)
    def _():
        def prologue(rows):
            h_ref[rows, :] = _rms_normalize(x_ref[rows, :], g_ref[...]).astype(BF16)
            o_ref[rows, :] = jnp.zeros((ROW_CHUNK, D_MODEL), F32)
        for_row_chunks(prologue)

    h = h_ref[...]
    gate = _dot(h, wg_ref[...])
    up = _dot(h, wu_ref[...])
    a = (gate * jax.nn.sigmoid(gate) * up).astype(BF16)
    for c in range(0, D_MODEL, out_cols):
        o_ref[:, c:c + out_cols] += _dot(a, wd_ref[:, c:c + out_cols])

    @pl.when(j == n_ff_steps - 1)
    def _():
        def epilogue(rows):
            y = x_ref[rows, :] + 0.5 * o_ref[rows, :]
            if fn_ref is not None:
                y = _rms_normalize(y, fn_ref[...])
            o_ref[rows, :] = y
        for_row_chunks(epilogue)


def _ffn_kernel_plain(x_ref, g_ref, wg_ref, wu_ref, wd_ref, o_ref, h_ref, **kw):
    _ffn_body(x_ref, g_ref, wg_ref, wu_ref, wd_ref, None, o_ref, h_ref, **kw)


def _ffn_kernel_final(x_ref, g_ref, wg_ref, wu_ref, wd_ref, fn_ref, o_ref, h_ref, **kw):
    _ffn_body(x_ref, g_ref, wg_ref, wu_ref, wd_ref, fn_ref, o_ref, h_ref, **kw)


def _ffn(x, gain, w_gate, w_up, w_down, final_gain, *, tm, tf, name):
    n_tok = x.shape[0]
    assert n_tok % tm == 0 and D_FF % tf == 0
    n_ff_steps = D_FF // tf
    kw = dict(n_ff_steps=n_ff_steps, out_cols=512)
    row_spec = pl.BlockSpec((tm, D_MODEL), lambda i, j: (i, 0))
    gain_spec = pl.BlockSpec((1, D_MODEL), lambda i, j: (0, 0))
    in_specs = [
        row_spec,
        gain_spec,
        pl.BlockSpec((D_MODEL, tf), lambda i, j: (0, j)),
        pl.BlockSpec((D_MODEL, tf), lambda i, j: (0, j)),
        pl.BlockSpec((tf, D_MODEL), lambda i, j: (j, 0)),
    ]
    args = [x, gain.reshape(1, D_MODEL), w_gate, w_up, w_down]
    if final_gain is None:
        body = functools.partial(_ffn_kernel_plain, **kw)
    else:
        body = functools.partial(_ffn_kernel_final, **kw)
        in_specs.append(gain_spec)
        args.append(final_gain.reshape(1, D_MODEL))
    return pl.pallas_call(
        body,
        grid=(n_tok // tm, n_ff_steps),
        in_specs=in_specs,
        out_specs=row_spec,
        out_shape=jax.ShapeDtypeStruct((n_tok, D_MODEL), F32),
        scratch_shapes=[pltpu.VMEM((tm, D_MODEL), BF16)],
        compiler_params=pltpu.CompilerParams(
            dimension_semantics=("arbitrary", "arbitrary"),
            vmem_limit_bytes=VMEM_LIMIT_BYTES,
        ),
        name=name,
    )(*args)


def _mix_body(x_ref, g_ref, win_ref, cw_ref, hist_ref, gv_ref, ws_ref, b_ref, gc_ref, gg_ref, wo_ref,
              o_ref, cs_ref, v_ref,
              h_ref, pb_ref, pc_ref, vb_ref, yn_ref, carry_ref, wm_ref,
              *, tm, seq_rows, tiles_per_seq):
    i = pl.program_id(0)
    j = pl.program_id(1)
    n_seq = tm // seq_rows
    mix_rows = min(GMLP_LEN, seq_rows)

    @pl.when(jnp.logical_and(i == 0, j == 0))
    def _():
        r = lax.broadcasted_iota(jnp.int32, (N_GMLP_GROUPS, GMLP_LEN, GMLP_LEN), 1) // CHUNK
        c = lax.broadcasted_iota(jnp.int32, (N_GMLP_GROUPS, GMLP_LEN, GMLP_LEN), 2) // CHUNK
        wm_ref[...] = jnp.where(r >= c, ws_ref[...], 0.0).astype(BF16)
        carry_ref[...] = jnp.zeros_like(carry_ref)

    @pl.when(j == 0)
    def _():
        h_ref[...] = _rms_normalize(x_ref[...], g_ref[...]).astype(BF16)
        pb_ref[...] = _dot(h_ref[...], win_ref[...])

    @pl.when(j == 1)
    def _():
        pc_ref[...] = _dot(h_ref[...], win_ref[...])

    @pl.when(j == 2)
    def _():
        z = pc_ref[...] * _dot(h_ref[...], win_ref[...])
        first = (i % tiles_per_seq) == 0
        hist = jnp.where(first, hist_ref[...], carry_ref[...])
        if n_seq == 1:
            h_old = hist[0, 0:1, :]
            h_new = hist[0, 1:2, :]
            z_last = z[tm - 2:tm, :].reshape(1, CONV_WIDTH - 1, D_CONV)
        else:
            shape3 = (n_seq, seq_rows, D_CONV)
            h_old = jnp.broadcast_to(hist[:, 0:1, :], shape3).reshape(tm, D_CONV)
            h_new = jnp.broadcast_to(hist[:, 1:2, :], shape3).reshape(tm, D_CONV)
            z_last = z.reshape(shape3)[:, seq_rows - 2:seq_rows, :]
        t = lax.broadcasted_iota(jnp.int32, (tm, D_CONV), 0) % seq_rows
        z1 = jnp.where(t == 0, h_new, pltpu.roll(z, 1, 0))
        z2 = jnp.where(t == 0, h_old, jnp.where(t == 1, h_new, pltpu.roll(z, 2, 0)))
        cw = cw_ref[...]
        y = pb_ref[...] * (cw[0:1, :] * z2 + cw[1:2, :] * z1 + cw[2:3, :] * z)
        yn_ref[:, 0:D_CONV] = _rms_normalize(y, gc_ref[...]).astype(BF16)
        cs_ref[...] = z_last
        carry_ref[...] = z_last

    @pl.when(j == 3)
    def _():
        v = _rms_normalize(jax.nn.gelu(_dot(h_ref[...], win_ref[...])), gv_ref[...])
        vb_ref[...] = v.astype(BF16)
        if v_ref is not None:
            v_ref[...] = v

    @pl.when(j == 4)
    def _():
        u = jax.nn.gelu(_dot(h_ref[...], win_ref[...]))
        for c in range(tm // mix_rows):
            rows = slice(c * mix_rows, (c + 1) * mix_rows)
            for g in range(N_GMLP_GROUPS):
                cols = slice(g * GMLP_GROUP_DIM, (g + 1) * GMLP_GROUP_DIM)
                mixed = _dot(wm_ref[g, 0:mix_rows, 0:mix_rows], vb_ref[rows, cols])
                pb_ref[rows, cols] = mixed + b_ref[0:mix_rows, cols]
        y = u * pb_ref[...]
        yn_ref[:, D_CONV:] = _rms_normalize(y, gg_ref[...]).astype(BF16)
        o_ref[...] = x_ref[...] + _dot(yn_ref[...], wo_ref[...])


def _mix_kernel_plain(x_ref, g_ref, win_ref, cw_ref, hist_ref, gv_ref, ws_ref, b_ref, gc_ref, gg_ref,
                      wo_ref, o_ref, cs_ref, *scratch, **kw):
    _mix_body(x_ref, g_ref, win_ref, cw_ref, hist_ref, gv_ref, ws_ref, b_ref, gc_ref, gg_ref, wo_ref,
              o_ref, cs_ref, None, *scratch, **kw)


def _mix_kernel_with_v(x_ref, g_ref, win_ref, cw_ref, hist_ref, gv_ref, ws_ref, b_ref, gc_ref, gg_ref,
                       wo_ref, o_ref, cs_ref, v_ref, *scratch, **kw):
    _mix_body(x_ref, g_ref, win_ref, cw_ref, hist_ref, gv_ref, ws_ref, b_ref, gc_ref, gg_ref, wo_ref,
              o_ref, cs_ref, v_ref, *scratch, **kw)


def _mixing(x, gain, w_in, conv_w, hist, v_gain, w_s, b_rows, conv_gain, gmlp_gain, w_o,
            *, tm, seq_len, emit_v, name):
    n_tok = x.shape[0]
    n_seqs = hist.shape[0]
    assert n_tok == n_seqs * seq_len and n_tok % tm == 0
    if seq_len >= tm:
        assert seq_len % tm == 0 and tm % GMLP_LEN == 0
        seq_rows, tiles_per_seq, seqs_per_tile = tm, seq_len // tm, 1
    else:
        assert tm % seq_len == 0 and seq_len <= GMLP_LEN and seq_len % 16 == 0
        seq_rows, tiles_per_seq, seqs_per_tile = seq_len, 1, tm // seq_len

    def col_block(i, j):
        return (0, jnp.where(j < 3, j, 7 - j))

    def seq_block(i, j):
        return (i // tiles_per_seq, 0, 0)

    def full(shape):
        return pl.BlockSpec(shape, lambda i, j: (0,) * len(shape))

    row_spec = pl.BlockSpec((tm, D_MODEL), lambda i, j: (i, 0))
    state_spec = pl.BlockSpec((seqs_per_tile, CONV_WIDTH - 1, D_CONV), seq_block)
    in_specs = [
        row_spec,
        full((1, D_MODEL)),
        pl.BlockSpec((D_MODEL, D_CONV), col_block),
        full((CONV_WIDTH, D_CONV)),
        state_spec,
        full((1, D_GMLP)),
        full((N_GMLP_GROUPS, GMLP_LEN, GMLP_LEN)),
        full((GMLP_LEN, D_GMLP)),
        full((1, D_CONV)),
        full((1, D_GMLP)),
        pl.BlockSpec(memory_space=pltpu.VMEM),
    ]
    out_specs = [row_spec, state_spec]
    out_shape = [
        jax.ShapeDtypeStruct((n_tok, D_MODEL), F32),
        jax.ShapeDtypeStruct((n_seqs, CONV_WIDTH - 1, D_CONV), F32),
    ]
    kw = dict(tm=tm, seq_rows=seq_rows, tiles_per_seq=tiles_per_seq)
    if emit_v:
        body = functools.partial(_mix_kernel_with_v, **kw)
        out_specs.append(pl.BlockSpec((tm, D_GMLP), lambda i, j: (i, 0)))
        out_shape.append(jax.ShapeDtypeStruct((n_tok, D_GMLP), F32))
    else:
        body = functools.partial(_mix_kernel_plain, **kw)
    return pl.pallas_call(
        body,
        grid=(n_tok // tm, N_MIX_STEPS),
        in_specs=in_specs,
        out_specs=out_specs,
        out_shape=out_shape,
        scratch_shapes=[
            pltpu.VMEM((tm, D_MODEL), BF16),
            pltpu.VMEM((tm, D_CONV), F32),
            pltpu.VMEM((tm, D_CONV), F32),
            pltpu.VMEM((tm, D_GMLP), BF16),
            pltpu.VMEM((tm, D_MODEL), BF16),
            pltpu.VMEM((seqs_per_tile, CONV_WIDTH - 1, D_CONV), F32),
            pltpu.VMEM((N_GMLP_GROUPS, GMLP_LEN, GMLP_LEN), BF16),
        ],
        compiler_params=pltpu.CompilerParams(
            dimension_semantics=("arbitrary", "arbitrary"),
            vmem_limit_bytes=VMEM_LIMIT_BYTES,
        ),
        name=name,
    )(x, gain.reshape(1, D_MODEL), w_in, conv_w, hist, v_gain.reshape(1, D_GMLP), w_s, b_rows,
      conv_gain.reshape(1, D_CONV), gmlp_gain.reshape(1, D_GMLP), w_o)


def kernel(x_prompt, x_sample, cache_conv, ffn1_norm, ffn1_w_gate, ffn1_w_up, ffn1_w_down, mix_norm, w_in,
           conv_w, gmlp_v_norm, gmlp_w_s, gmlp_b, conv_out_norm, gmlp_out_norm, w_o, ffn2_norm,
           ffn2_w_gate, ffn2_w_up, ffn2_w_down, final_norm):
    depth = ffn1_norm.shape[0]
    assert depth == 1
    l = 0
    w1 = [w[l].astype(BF16) for w in (ffn1_w_gate, ffn1_w_up, ffn1_w_down)]
    w2 = [w[l].astype(BF16) for w in (ffn2_w_gate, ffn2_w_up, ffn2_w_down)]
    w_in_b = w_in[l].astype(BF16)
    w_o_b = w_o[l].astype(BF16)
    b_rows = jnp.repeat(gmlp_b[l].T, GMLP_GROUP_DIM, axis=1)

    def trunk(x3, hist, *, tm_ffn, tm_mix, emit_v, tag):
        n_seqs, seq_len, _ = x3.shape
        x = x3.reshape(n_seqs * seq_len, D_MODEL)
        x = _ffn(x, ffn1_norm[l], *w1, None, tm=tm_ffn, tf=512, name=f"ffn1_{tag}")
        outs = _mixing(x, mix_norm[l], w_in_b, conv_w[l], hist, gmlp_v_norm[l], gmlp_w_s[l], b_rows,
                       conv_out_norm[l], gmlp_out_norm[l], w_o_b,
                       tm=tm_mix, seq_len=seq_len, emit_v=emit_v, name=f"mix_{tag}")
        x = _ffn(outs[0], ffn2_norm[l], *w2, final_norm, tm=tm_ffn, tf=512, name=f"ffn2_{tag}")
        return (x.reshape(n_seqs, seq_len, D_MODEL),) + tuple(outs[1:])

    zero_hist = jnp.zeros((x_prompt.shape[0], CONV_WIDTH - 1, D_CONV), F32)
    y_prompt, state_prompt = trunk(x_prompt, zero_hist, tm_ffn=1024, tm_mix=512, emit_v=False, tag="prompt")
    n_dec = x_sample.shape[0] * x_sample.shape[1]
    y_sample, state_sample, v_sample = trunk(x_sample, cache_conv[l], tm_ffn=n_dec, tm_mix=n_dec,
                                             emit_v=True, tag="sample")
    return (y_prompt, y_sample, state_prompt[None], state_sample[None],
            v_sample.reshape(1, x_sample.shape[0], x_sample.shape[1], D_GMLP))
```
